```python
import jax, jax.numpy as jnp
from jax import lax
import numpy as np

D_MODEL = 1024
BATCH = 16
SEQ = 256
DEPTH = 2
DEC_BATCH = 8
DEC_SEQ = 4096
PAST_LEN = 256

GRID_W = 64
DH = 64
NA_HEADS = 8
NA_WIN_ROWS = 8
NA_WIN_COLS = 16
GQA_HEADS = 8
GQA_KV = 2
SWA_HEADS = 8
SWA_KV = 2
SWA_WINDOW = 128
ML_HEADS = 4
ML_DH = 128
ML_CHUNK = 64
Q_BLOCK = 128
N_BRANCH = 4
BRANCH_W = NA_HEADS * DH
D_FF = ((8 * D_MODEL // 3 + 255) // 256) * 256
ROPE_BASE = 10000.0
NORM_EPS = 1e-6
IN_SIZES = (NA_HEADS * DH, NA_HEADS * DH, NA_HEADS * DH,
            GQA_HEADS * DH, GQA_KV * DH, GQA_KV * DH,
            SWA_HEADS * DH, SWA_KV * DH, SWA_KV * DH,
            ML_HEADS * ML_DH, ML_HEADS * ML_DH, ML_HEADS * ML_DH, ML_HEADS * ML_DH, 2 * ML_HEADS, 2 * ML_HEADS,
            N_BRANCH * D_MODEL)
N_IN = sum(IN_SIZES)
ML_F_OFFSET = sum(IN_SIZES[:14])

kernel_name = 'hybrid_flow_trunk_step'


def _rmsnorm(x, g):
    xf = x.astype(jnp.float32)
    y = xf * lax.rsqrt(jnp.mean(xf * xf, axis=-1, keepdims=True) + NORM_EPS)
    return (y * g.astype(jnp.float32)).astype(x.dtype)


def _headnorm(t, g):
    return t * lax.rsqrt(jnp.mean(t * t, axis=-1, keepdims=True) + NORM_EPS) * g.astype(jnp.float32)


def _adaln(cvec, w, b):
    m = jnp.dot(jax.nn.silu(cvec), w) + b
    return jnp.split(m, 6, axis=-1)


def _modulate(xn, shift, scale):
    return xn * (1 + scale) + shift


def _rope2d(x):
    S = x.shape[1]
    half = DH // 2
    t = jnp.arange(S)
    freqs = ROPE_BASE ** (-jnp.arange(0, half, 2, dtype=jnp.float32) / half)

    def rot(xa, pos):
        ang = pos.astype(jnp.float32)[:, None] * freqs
        cos = jnp.cos(ang)[None, :, None, :]
        sin = jnp.sin(ang)[None, :, None, :]
        x1, x2 = jnp.split(xa, 2, axis=-1)
        return jnp.concatenate([x1 * cos - x2 * sin, x2 * cos + x1 * sin], axis=-1)

    return jnp.concatenate([rot(x[..., :half], t // GRID_W), rot(x[..., half:], t % GRID_W)], axis=-1)


def _branch_inputs(h, w_in, b_in, q_g, k_g, latent):
    z = (jnp.dot(h, w_in) + b_in).astype(jnp.float32)
    parts = []
    off = 0
    for size in IN_SIZES:
        parts.append(z[..., off:off + size])
        off += size
    (na_q, na_k, na_v, g_q, g_k, g_v, s_q, s_k, s_v, m_q, m_k, m_v, m_o, m_i, m_f, gates) = parts

    def heads(t, n):
        return t.reshape(t.shape[:-1] + (n, t.shape[-1] // n))

    g_q = _headnorm(heads(g_q, GQA_HEADS), q_g)
    g_k = _headnorm(heads(g_k, GQA_KV), k_g)
    s_q, s_k = heads(s_q, SWA_HEADS), heads(s_k, SWA_KV)
    if latent:
        g_q, g_k, s_q, s_k = _rope2d(g_q), _rope2d(g_k), _rope2d(s_q), _rope2d(s_k)
    lead = z.shape[:-1]
    return (heads(na_q, NA_HEADS), heads(na_k, NA_HEADS), heads(na_v, NA_HEADS),
            g_q, g_k, heads(g_v, GQA_KV),
            s_q, s_k, heads(s_v, SWA_KV),
            heads(m_q, ML_HEADS), heads(m_k, ML_HEADS) * ML_DH ** -0.5, heads(m_v, ML_HEADS), heads(m_o, ML_HEADS),
            m_i.reshape(lead + (2, ML_HEADS)),
            jax.nn.log_sigmoid(m_f).reshape(lead + (2, ML_HEADS)),
            jax.nn.sigmoid(gates).reshape(lead + (N_BRANCH, D_MODEL)))


def _attn_full(q, k, v, sink):
    B, L, HQ, _ = q.shape
    KV = k.shape[2]
    G = HQ // KV
    s = jnp.einsum('bqkgd,blkd->bkgql', q.reshape(B, L, KV, G, DH), k) * DH ** -0.5
    if sink is not None:
        s = jnp.concatenate([s, jnp.broadcast_to(sink.astype(jnp.float32).reshape(1, KV, G, 1, 1), s.shape[:-1] + (1,))], axis=-1)
    p = jax.nn.softmax(s, axis=-1)[..., :L]
    o = jnp.einsum('bkgql,blkd->bqkgd', p, v)
    return o.reshape(B, L, HQ * DH)


def _natten_latent(q, k, v, kc, vc, rpb):
    B, S, H, _ = q.shape
    rows = S // GRID_W
    wr = min(NA_WIN_ROWS, rows)
    wc = NA_WIN_COLS
    scale = DH ** -0.5
    qg = jnp.moveaxis(q.reshape(B, rows, GRID_W, H, DH), 1, 0)
    kg = k.reshape(B, rows, GRID_W, H, DH)
    vg = v.reshape(B, rows, GRID_W, H, DH)
    r_idx = jnp.arange(rows)
    row_start = jnp.clip(r_idx - wr // 2, 0, rows - wr)
    c_idx = jnp.arange(GRID_W)
    col_keys = jnp.clip(c_idx - wc // 2, 0, GRID_W - wc)[:, None] + jnp.arange(wc)[None, :]
    col_bias_idx = col_keys - c_idx[:, None] + (NA_WIN_COLS - 1)
    rpb = rpb.astype(jnp.float32)

    def row_block(args):
        q_r, r, rs = args
        k_sel = lax.dynamic_slice_in_dim(kg, rs, wr, axis=1)[:, :, col_keys]
        v_sel = lax.dynamic_slice_in_dim(vg, rs, wr, axis=1)[:, :, col_keys]
        row_bias_idx = rs + jnp.arange(wr) - r + (NA_WIN_ROWS - 1)
        bias = rpb[:, row_bias_idx[:, None, None], col_bias_idx[None, :, :]]
        s_nb = jnp.einsum('bqhd,brqchd->bhqrc', q_r, k_sel) * scale + jnp.transpose(bias, (0, 2, 1, 3))[None]
        s_nb = s_nb.reshape(B, H, GRID_W, wr * wc)
        s_ctx = jnp.einsum('bqhd,blhd->bhql', q_r, kc) * scale
        p = jax.nn.softmax(jnp.concatenate([s_nb, s_ctx], axis=-1), axis=-1)
        p_nb = p[..., :wr * wc].reshape(B, H, GRID_W, wr, wc)
        return (jnp.einsum('bhqrc,brqchd->bqhd', p_nb, v_sel)
                + jnp.einsum('bhql,blhd->bqhd', p[..., wr * wc:], vc))

    out = lax.map(row_block, (qg, r_idx, row_start))
    return jnp.moveaxis(out, 0, 1).reshape(B, S, H * DH)


def _gqa_dense_latent(q, k, v, kc, vc):
    B, S, HQ, _ = q.shape
    KV = k.shape[2]
    G = HQ // KV
    keys = jnp.concatenate([kc, k], axis=1)
    vals = jnp.concatenate([vc, v], axis=1)
    qb = jnp.moveaxis(q.reshape(B, S // Q_BLOCK, Q_BLOCK, KV, G, DH), 1, 0)

    def block(qi):
        s = jnp.einsum('bqkgd,bnkd->bkgqn', qi, keys) * DH ** -0.5
        return jnp.einsum('bkgqn,bnkd->bqkgd', jax.nn.softmax(s, axis=-1), vals)

    o = lax.map(block, qb)
    return jnp.moveaxis(o, 0, 1).reshape(B, S, HQ * DH)


def _swa_latent(q, k, v, kc, vc, sink):
    B, S, HQ, _ = q.shape
    KV = k.shape[2]
    G = HQ // KV
    BLK = SWA_WINDOW
    nb = S // BLK
    L = kc.shape[1]
    scale = DH ** -0.5

    def band(t):
        tb = jnp.pad(t, ((0, 0), (BLK, BLK), (0, 0), (0, 0))).reshape(B, nb + 2, BLK, KV, DH)
        return jnp.concatenate([tb[:, :-2], tb[:, 1:-1], tb[:, 2:]], axis=2)

    kb, vb = band(k), band(v)
    qb = q.reshape(B, nb, BLK, KV, G, DH)
    s_band = jnp.einsum('bnqkgd,bnjkd->bnkgqj', qb, kb) * scale
    blk = jnp.arange(nb)[:, None, None] * BLK
    qpos = blk + jnp.arange(BLK)[None, :, None]
    kpos = blk - BLK + jnp.arange(3 * BLK)[None, None, :]
    valid = (jnp.abs(qpos - kpos) <= SWA_WINDOW) & (kpos >= 0) & (kpos < S)
    s_band = jnp.where(valid[None, :, None, None], s_band, -jnp.inf)
    s_ctx = jnp.einsum('bnqkgd,blkd->bnkgql', qb, kc) * scale
    s_sink = jnp.broadcast_to(sink.astype(jnp.float32).reshape(1, 1, KV, G, 1, 1), s_ctx.shape[:-1] + (1,))
    p = jax.nn.softmax(jnp.concatenate([s_band, s_ctx, s_sink], axis=-1), axis=-1)
    nbk = 3 * BLK
    o = (jnp.einsum('bnkgqj,bnjkd->bnqkgd', p[..., :nbk], vb)
         + jnp.einsum('bnkgql,blkd->bnqkgd', p[..., nbk:nbk + L], vc))
    return o.reshape(B, S, HQ * DH)


def _mlstm_scan(q, k, v, i_pre, log_f, C0, n0, m0):
    B, S, H, d = q.shape
    nc = S // ML_CHUNK

    def chunks(t):
        t = t.reshape((B, nc, ML_CHUNK) + t.shape[2:])
        return jnp.moveaxis(jnp.moveaxis(t, 1, 0), 2, 3)

    tril = jnp.tril(jnp.ones((ML_CHUNK, ML_CHUNK), dtype=bool))

    def step(carry, xs):
        C, n, m = carry
        qc, kc, vc, ic, fc = xs
        b = jnp.cumsum(fc, axis=-1)
        Dm = jnp.where(tril, b[..., :, None] - b[..., None, :] + ic[..., None, :], -jnp.inf)
        inter = b + m[..., None]
        m_t = jnp.maximum(inter, jnp.max(Dm, axis=-1))
        w_intra = jnp.exp(Dm - m_t[..., None])
        w_inter = jnp.exp(inter - m_t)
        s = jnp.einsum('bhtd,bhsd->bhts', qc, kc) * w_intra
        num = w_inter[..., None] * jnp.einsum('bhtd,bhde->bhte', qc, C) + jnp.einsum('bhts,bhse->bhte', s, vc)
        den = w_inter * jnp.einsum('bhtd,bhd->bht', qc, n) + jnp.sum(s, axis=-1)
        h = num / jnp.maximum(jnp.abs(den), jnp.exp(-m_t))[..., None]
        b_last = b[..., -1]
        g = b_last[..., None] - b + ic
        m_new = jnp.maximum(b_last + m, jnp.max(g, axis=-1))
        w_s = jnp.exp(g - m_new[..., None])
        decay = jnp.exp(b_last + m - m_new)
        C_new = decay[..., None, None] * C + jnp.einsum('bhs,bhsd,bhse->bhde', w_s, kc, vc)
        n_new = decay[..., None] * n + jnp.einsum('bhs,bhsd->bhd', w_s, kc)
        return (C_new, n_new, m_new), h

    (C, n, m), h = lax.scan(step, (C0, n0, m0), tuple(chunks(t) for t in (q, k, v, i_pre, log_f)))
    h = jnp.moveaxis(jnp.moveaxis(h, 0, 1), 2, 3).reshape(B, S, H, d)
    return h, (C, n, m)


def _mlstm_bidir(q, k, v, i_pre, log_f, state_f, state_b):
    h_f, st_f = _mlstm_scan(q, k, v, i_pre[:, :, 0], log_f[:, :, 0], *state_f)
    rev = lambda t: jnp.flip(t, axis=1)
    h_b, st_b = _mlstm_scan(rev(q), rev(k), rev(v), rev(i_pre[:, :, 1]), rev(log_f[:, :, 1]), *state_b)
    return h_f + rev(h_b), st_f, st_b


def _mlstm_out(h, o_pre, g):
    y = h * jax.nn.sigmoid(o_pre)
    y = _headnorm(y, g.reshape(ML_HEADS, ML_DH))
    return y.reshape(y.shape[:-2] + (ML_HEADS * ML_DH,))


def _merge(outs, gates, w_b, w_o, dtype):
    merged = gates[..., 0, :] * jnp.dot(outs[0], w_b[0].astype(jnp.float32))
    for i in range(1, N_BRANCH):
        merged = merged + gates[..., i, :] * jnp.dot(outs[i], w_b[i].astype(jnp.float32))
    return jnp.dot(merged, w_o.astype(jnp.float32)).astype(dtype)


def _swiglu(h, w1, w2):
    a = jnp.dot(h, w1)
    gt, up = jnp.split(a, 2, axis=-1)
    return jnp.dot(jax.nn.silu(gt) * up, w2)


def setup_inputs(seed: int = 0) -> dict:
    key = jax.random.key(seed)
    ks = jax.random.split(key, 32)
    f32 = jnp.float32

    def nrm(k, shape, s):
        return jax.random.normal(k, shape, f32) * s

    f_bias = jnp.linspace(3.0, 6.0, 2 * ML_HEADS, dtype=f32)
    b_in = nrm(ks[15], (DEPTH, N_IN), 0.02).at[:, ML_F_OFFSET:ML_F_OFFSET + 2 * ML_HEADS].add(f_bias)
    return {
        'x_prompt': nrm(ks[0], (BATCH, SEQ, D_MODEL), 1.0),
        'x_sample': nrm(ks[1], (DEC_BATCH, DEC_SEQ, D_MODEL), 1.0),
        'cache_na_kv': nrm(ks[2], (DEC_BATCH, DEPTH, 2, PAST_LEN, NA_HEADS, DH), 1.0),
        'cache_gqa_kv': nrm(ks[3], (DEC_BATCH, DEPTH, 2, PAST_LEN, GQA_KV, DH), 1.0),
        'cache_swa_kv': nrm(ks[4], (DEC_BATCH, DEPTH, 2, PAST_LEN, SWA_KV, DH), 1.0),
        'state_mlstm_C': nrm(ks[5], (DEC_BATCH, DEPTH, 2, ML_HEADS, ML_DH, ML_DH), 0.1),
        'state_mlstm_n': nrm(ks[6], (DEC_BATCH, DEPTH, 2, ML_HEADS, ML_DH), 0.1),
        'state_mlstm_m': nrm(ks[7], (DEC_BATCH, DEPTH, 2, ML_HEADS), 1.0),
        'c': nrm(ks[8], (DEC_BATCH, D_MODEL), 1.0),
        'c_ctx': nrm(ks[9], (D_MODEL,), 1.0),
        'w_mod': nrm(ks[10], (DEPTH, D_MODEL, 6 * D_MODEL), 0.5 * D_MODEL ** -0.5),
        'b_mod': nrm(ks[11], (DEPTH, 6 * D_MODEL), 0.02),
        'norm1_g': 1.0 + nrm(ks[12], (DEPTH, D_MODEL), 0.02),
        'norm2_g': 1.0 + nrm(ks[13], (DEPTH, D_MODEL), 0.02),
        'w_in': nrm(ks[14], (DEPTH, D_MODEL, N_IN), D_MODEL ** -0.5),
        'b_in': b_in,
        'na_rpb': nrm(ks[16], (DEPTH, NA_HEADS, 2 * NA_WIN_ROWS - 1, 2 * NA_WIN_COLS - 1), 0.1),
        'gqa_q_g': 1.0 + nrm(ks[17], (DEPTH, DH), 0.02),
        'gqa_k_g': 1.0 + nrm(ks[18], (DEPTH, DH), 0.02),
        'swa_sink': nrm(ks[19], (DEPTH, SWA_HEADS), 1.0),
        'mlstm_norm_g': 1.0 + nrm(ks[20], (DEPTH, ML_HEADS * ML_DH), 0.02),
        'w_branch': nrm(ks[21], (DEPTH, N_BRANCH, BRANCH_W, D_MODEL), BRANCH_W ** -0.5),
        'w_out': nrm(ks[22], (DEPTH, D_MODEL, D_MODEL), D_MODEL ** -0.5),
        'w_ffn_in': nrm(ks[23], (DEPTH, D_MODEL, 2 * D_FF), D_MODEL ** -0.5),
        'w_ffn_out': nrm(ks[24], (DEPTH, D_FF, D_MODEL), D_FF ** -0.5),
        'final_norm_g': 1.0 + nrm(ks[25], (D_MODEL,), 0.02),
    }


def reference(x_prompt, x_sample, cache_na_kv, cache_gqa_kv, cache_swa_kv, state_mlstm_C, state_mlstm_n,
              state_mlstm_m, c, c_ctx, w_mod, b_mod, norm1_g, norm2_g, w_in, b_in, na_rpb, gqa_q_g, gqa_k_g,
              swa_sink, mlstm_norm_g, w_branch, w_out, w_ffn_in, w_ffn_out, final_norm_g):
    f32 = jnp.float32
    Bp = x_prompt.shape[0]
    zero_state = (jnp.zeros((Bp, ML_HEADS, ML_DH, ML_DH), f32), jnp.zeros((Bp, ML_HEADS, ML_DH), f32),
                  jnp.zeros((Bp, ML_HEADS), f32))
    xp, xs = x_prompt, x_sample
    na_list, gqa_list, swa_list, C_list, n_list, m_list = [], [], [], [], [], []
    for l in range(DEPTH):
        sh1, sc1, gt1, sh2, sc2, gt2 = _adaln(c_ctx, w_mod[l], b_mod[l])
        h = _modulate(_rmsnorm(xp, norm1_g[l]), sh1, sc1)
        (na_q, na_k, na_v, g_q, g_k, g_v, s_q, s_k, s_v, m_q, m_k, m_v, m_o, m_i, m_f, gates) = \
            _branch_inputs(h, w_in[l], b_in[l], gqa_q_g[l], gqa_k_g[l], False)
        o_na = _attn_full(na_q, na_k, na_v, None)
        o_gqa = _attn_full(g_q, g_k, g_v, None)
        o_swa = _attn_full(s_q, s_k, s_v, swa_sink[l])
        h_ml, st_f, st_b = _mlstm_bidir(m_q, m_k, m_v, m_i, m_f, zero_state, zero_state)
        o_ml = _mlstm_out(h_ml, m_o, mlstm_norm_g[l])
        xp = xp + gt1 * _merge((o_na, o_gqa, o_swa, o_ml), gates, w_branch[l], w_out[l], xp.dtype)
        xp = xp + gt2 * _swiglu(_modulate(_rmsnorm(xp, norm2_g[l]), sh2, sc2), w_ffn_in[l], w_ffn_out[l])
        na_list.append(jnp.stack([na_k, na_v], axis=1))
        gqa_list.append(jnp.stack([g_k, g_v], axis=1))
        swa_list.append(jnp.stack([s_k, s_v], axis=1))
        C_list.append(jnp.stack([st_f[0], st_b[0]], axis=1))
        n_list.append(jnp.stack([st_f[1], st_b[1]], axis=1))
        m_list.append(jnp.stack([st_f[2], st_b[2]], axis=1))

        sh1, sc1, gt1, sh2, sc2, gt2 = [m[:, None, :] for m in _adaln(c, w_mod[l], b_mod[l])]
        h = _modulate(_rmsnorm(xs, norm1_g[l]), sh1, sc1)
        (na_q, na_k, na_v, g_q, g_k, g_v, s_q, s_k, s_v, m_q, m_k, m_v, m_o, m_i, m_f, gates) = \
            _branch_inputs(h, w_in[l], b_in[l], gqa_q_g[l], gqa_k_g[l], True)
        kv_na = cache_na_kv[:, l].astype(f32)
        kv_g = cache_gqa_kv[:, l].astype(f32)
        kv_s = cache_swa_kv[:, l].astype(f32)
        o_na = _natten_latent(na_q, na_k, na_v, kv_na[:, 0], kv_na[:, 1], na_rpb[l])
        o_gqa = _gqa_dense_latent(g_q, g_k, g_v, kv_g[:, 0], kv_g[:, 1])
        o_swa = _swa_latent(s_q, s_k, s_v, kv_s[:, 0], kv_s[:, 1], swa_sink[l])
        lat_f = (state_mlstm_C[:, l, 0].astype(f32), state_mlstm_n[:, l, 0].astype(f32), state_mlstm_m[:, l, 0].astype(f32))
        lat_b = (state_mlstm_C[:, l, 1].astype(f32), state_mlstm_n[:, l, 1].astype(f32), state_mlstm_m[:, l, 1].astype(f32))
        h_ml, _, _ = _mlstm_bidir(m_q, m_k, m_v, m_i, m_f, lat_f, lat_b)
        o_ml = _mlstm_out(h_ml, m_o, mlstm_norm_g[l])
        xs = xs + gt1 * _merge((o_na, o_gqa, o_swa, o_ml), gates, w_branch[l], w_out[l], xs.dtype)
        xs = xs + gt2 * _swiglu(_modulate(_rmsnorm(xs, norm2_g[l]), sh2, sc2), w_ffn_in[l], w_ffn_out[l])

    y_prompt = _rmsnorm(xp, final_norm_g)
    y_sample = _rmsnorm(xs, final_norm_g)
    new_na_kv = jnp.stack(na_list, axis=1)
    new_gqa_kv = jnp.stack(gqa_list, axis=1)
    new_swa_kv = jnp.stack(swa_list, axis=1)
    new_ml_C = jnp.stack(C_list, axis=1)
    new_ml_n = jnp.stack(n_list, axis=1)
    new_ml_m = jnp.stack(m_list, axis=1)
    return (y_prompt, y_sample, new_na_kv, new_gqa_kv, new_swa_kv, new_ml_C, new_ml_n, new_ml_m)
```

```python
import functools

import numpy as np
import jax
import jax.numpy as jnp
from jax import lax
from jax.experimental import pallas as pl
from jax.experimental.pallas import tpu as pltpu

F32 = jnp.float32
BF16 = jnp.bfloat16

D_MODEL = 1024
GRID_W = 64
DH = 64
N_HEADS = 8
NA_WIN_ROWS = 8
NA_WIN_COLS = 16
SWA_WINDOW = 128
ML_HEADS = 4
ML_DH = 128
ML_CHUNK = 64
N_BRANCH = 4
BRANCH_W = N_HEADS * DH
D_FF = ((8 * D_MODEL // 3 + 255) // 256) * 256
ROPE_BASE = 10000.0
NORM_EPS = 1e-6
LANES = 128
NEG_BIG = -1e30

_OFF_NA_Q, _OFF_NA_K, _OFF_NA_V = 0, 512, 1024
_OFF_G_Q, _OFF_G_K, _OFF_G_V = 1536, 2048, 2176
_OFF_S_Q, _OFF_S_K, _OFF_S_V = 2304, 2816, 2944
_OFF_M_Q, _OFF_M_K, _OFF_M_V, _OFF_M_O = 3072, 3584, 4096, 4608
_OFF_M_I, _OFF_M_F, _OFF_GATES = 5120, 5128, 5136

TN = 512
N_MAIN = 18 * TN
_TILE_GQ, _TILE_KV, _TILE_SQ, _TILE_GATES = 3, 4, 5, 10
_QPERM = (0, 4, 1, 5, 2, 6, 3, 7)

_VMEM_LIMIT = 52 * 1024 * 1024


def _packed_columns():
    r = np.arange
    qperm = np.concatenate([h * DH + r(DH) for h in _QPERM])
    cols = np.concatenate([
        _OFF_NA_Q + r(512), _OFF_NA_K + r(512), _OFF_NA_V + r(512),
        _OFF_G_Q + qperm,
        _OFF_G_K + r(128), _OFF_G_V + r(128), _OFF_S_K + r(128), _OFF_S_V + r(128),
        _OFF_S_Q + qperm,
        _OFF_M_Q + r(512), _OFF_M_K + r(512), _OFF_M_V + r(512), _OFF_M_O + r(512),
        _OFF_GATES + r(N_BRANCH * D_MODEL)])
    scale = np.ones((N_MAIN,), np.float32)
    scale[0:512] = DH ** -0.5
    scale[5 * TN:6 * TN] = DH ** -0.5
    scale[7 * TN:8 * TN] = ML_DH ** -0.5
    gcols = np.zeros((LANES,), np.int64)
    gmask = np.zeros((LANES,), np.float32)
    for h in range(ML_HEADS):
        for k, c in enumerate((_OFF_M_I + h, _OFF_M_I + ML_HEADS + h, _OFF_M_F + h, _OFF_M_F + ML_HEADS + h)):
            gcols[h * 32 + k] = c
            gmask[h * 32 + k] = 1.0
    return cols, scale, gcols, gmask


_COLS, _COLSCALE, _GCOLS, _GMASK = _packed_columns()


def _dot(a, b):
    return jnp.dot(a, b, preferred_element_type=F32)


def _dot_nt(a, b):
    return lax.dot_general(a, b, (((1,), (1,)), ((), ())), preferred_element_type=F32)


def _split_dot(x, p):
    hi = x.astype(BF16)
    lo = (x - hi.astype(F32)).astype(BF16)
    return _dot(hi, p) + _dot(lo, p)


def _params(sem):
    return pltpu.CompilerParams(dimension_semantics=sem, vmem_limit_bytes=_VMEM_LIMIT)


def _mod_kernel(c_ref, w_ref, b_ref, o_ref):
    c = c_ref[...]
    a = c * jax.nn.sigmoid(c)
    w = w_ref[...]
    a_hi = a.astype(BF16)
    a_lo = (a - a_hi.astype(F32)).astype(BF16)
    w_hi = w.astype(BF16)
    w_lo = (w - w_hi.astype(F32)).astype(BF16)
    o_ref[...] = _dot(a_hi, w_hi) + _dot(a_lo, w_hi) + _dot(a_hi, w_lo) + b_ref[...]


def _modulation(cvecs, w_mod, b_mod):
    depth, _, n = w_mod.shape
    rows = cvecs.shape[0]
    tn = 768
    return pl.pallas_call(
        _mod_kernel,
        grid=(depth, n // tn),
        in_specs=[pl.BlockSpec((rows, D_MODEL), lambda l, j: (0, 0)),
                  pl.BlockSpec((None, D_MODEL, tn), lambda l, j: (l, 0, j)),
                  pl.BlockSpec((None, 1, tn), lambda l, j: (l, 0, j))],
        out_specs=pl.BlockSpec((None, rows, tn), lambda l, j: (l, 0, j)),
        out_shape=jax.ShapeDtypeStruct((depth, rows, n), F32),
        compiler_params=_params(("arbitrary", "arbitrary")),
        name="adaln_mod",
    )(cvecs, w_mod, b_mod.reshape(depth, 1, n))


def _rope(y, cos, sa, sb):
    return y * cos + pltpu.roll(y, LANES - 16, 1) * sa + pltpu.roll(y, 16, 1) * sb


def _log_sigmoid(x):
    return jnp.minimum(x, 0.0) - jnp.log1p(jnp.exp(-jnp.abs(x)))


def _chunk_cumsums(v, tm):
    pos = lax.broadcasted_iota(jnp.int32, v.shape, 0) % ML_CHUNK
    pre, suf = v, v
    s = 1
    while s < ML_CHUNK:
        pre = pre + jnp.where(pos >= s, pltpu.roll(pre, s, 0), 0.0)
        suf = suf + jnp.where(pos < ML_CHUNK - s, pltpu.roll(suf, tm - s, 0), 0.0)
        s *= 2
    return pre, suf


def _inproj_kernel(x_ref, g_ref, sh_ref, sc_ref, w_ref, b_ref, cs_ref, wif_ref, bif_ref, p_ref, qg_ref, kg_ref,
                   cos_ref, sa_ref, sb_ref, z_ref, gate_ref, h_scr, *, latent, tm):
    j = pl.program_id(1)

    @pl.when(j == 0)
    def _():
        x = x_ref[...]
        y = x * lax.rsqrt(jnp.mean(x * x, axis=-1, keepdims=True) + NORM_EPS) * g_ref[...]
        hb = (y * (1.0 + sc_ref[...]) + sh_ref[...]).astype(BF16)
        h_scr[...] = hb
        zf = _dot(hb, wif_ref[...]) + bif_ref[...]
        kind = lax.broadcasted_iota(jnp.int32, zf.shape, 1) % 32
        lf = jnp.where((kind == 2) | (kind == 3), _log_sigmoid(zf), 0.0)
        pre, suf = _chunk_cumsums(lf, tm)
        gate_ref[...] = jnp.where(kind < 2, zf, jnp.where(kind == 2, pre, suf))

    acc = (_dot(h_scr[...], w_ref[...]) + b_ref[...]) * cs_ref[...]

    def rope_chunk(y, c):
        if not latent:
            return y
        del c
        return _rope(y, cos_ref[...], sa_ref[...], sb_ref[...])

    def chunks(y):
        return [y[:, c * LANES:(c + 1) * LANES] for c in range(TN // LANES)]

    @pl.when(j == _TILE_GQ)
    def _():
        y = acc * lax.rsqrt(_split_dot(acc * acc, p_ref[...]) + NORM_EPS) * qg_ref[...]
        z_ref[...] = jnp.concatenate([rope_chunk(c, i) for i, c in enumerate(chunks(y))], axis=1).astype(z_ref.dtype)

    @pl.when(j == _TILE_KV)
    def _():
        gk, gv, sk, sv = chunks(acc)
        gk = gk * lax.rsqrt(_split_dot(gk * gk, p_ref[0:LANES, 0:LANES]) + NORM_EPS) * kg_ref[...]
        z_ref[...] = jnp.concatenate([rope_chunk(gk, 0), gv, rope_chunk(sk, 2), sv], axis=1).astype(z_ref.dtype)

    @pl.when(j == _TILE_SQ)
    def _():
        z_ref[...] = jnp.concatenate([rope_chunk(c, i) for i, c in enumerate(chunks(acc))], axis=1).astype(z_ref.dtype)

    @pl.when(j >= _TILE_GATES)
    def _():
        z_ref[...] = jax.nn.sigmoid(acc).astype(z_ref.dtype)

    @pl.when((j < _TILE_GQ) | ((j > _TILE_SQ) & (j < _TILE_GATES)))
    def _():
        z_ref[...] = acc.astype(z_ref.dtype)


def _in_projection(x, gain, shift, scale, lw, rope, seq, latent, out_dtype):
    t = x.shape[0]
    tm = min(1024, seq) if latent else min(1024, t)
    per_mod = (t // shift.shape[0]) // tm
    per_seq = max(seq // tm, 1)
    const = lambda i, j: (0, 0)
    mod = lambda i, j: (i // per_mod, 0, 0)
    tab = lambda i, j: (i % per_seq, 0)
    cos, sa, sb = rope
    return pl.pallas_call(
        functools.partial(_inproj_kernel, latent=latent, tm=tm),
        grid=(t // tm, N_MAIN // TN),
        in_specs=[pl.BlockSpec((tm, D_MODEL), lambda i, j: (i, 0)),
                  pl.BlockSpec((1, D_MODEL), const),
                  pl.BlockSpec((None, 1, D_MODEL), mod),
                  pl.BlockSpec((None, 1, D_MODEL), mod),
                  pl.BlockSpec((D_MODEL, TN), lambda i, j: (0, j)),
                  pl.BlockSpec((1, TN), lambda i, j: (0, j)),
                  pl.BlockSpec((1, TN), lambda i, j: (0, j)),
                  pl.BlockSpec((D_MODEL, LANES), const),
                  pl.BlockSpec((1, LANES), const),
                  pl.BlockSpec((TN, TN), const),
                  pl.BlockSpec((1, TN), const),
                  pl.BlockSpec((1, LANES), const),
                  pl.BlockSpec((tm, LANES), tab),
                  pl.BlockSpec((tm, LANES), tab),
                  pl.BlockSpec((tm, LANES), tab)],
        out_specs=[pl.BlockSpec((tm, TN), lambda i, j: (i, j)),
                   pl.BlockSpec((tm, LANES), lambda i, j: (i, 0))],
        out_shape=[jax.ShapeDtypeStruct((t, N_MAIN), out_dtype),
                   jax.ShapeDtypeStruct((t, LANES), F32)],
        scratch_shapes=[pltpu.VMEM((tm, D_MODEL), BF16)],
        compiler_params=_params(("arbitrary", "arbitrary")),
        name="in_projection",
    )(x, gain, shift, scale, lw["w_main"], lw["b_main"], lw["colscale"], lw["w_if"], lw["b_if"], lw["pmat"],
      lw["q_gain"], lw["k_gain"], cos, sa, sb)


def _stack_heads(q):
    lane = lax.broadcasted_iota(jnp.int32, q.shape, 1)
    zero = jnp.zeros_like(q)
    return jnp.concatenate([jnp.where(lane < DH, q, zero), jnp.where(lane >= DH, q, zero)], axis=0)


def _unstack_heads(o, rows):
    lane = lax.broadcasted_iota(jnp.int32, (rows, LANES), 1)
    return jnp.where(lane < DH, o[:rows], o[rows:])


def _sink_column(sink_ref, rows):
    sk = sink_ref[...]
    r = lax.broadcasted_iota(jnp.int32, (2 * rows, 1), 0)
    return jnp.where(r < rows, sk[0:1, 0:1], sk[1:2, 0:1])


def _softmax_pv(scores, values, sink):
    m = jnp.max(scores[0], axis=-1, keepdims=True)
    for s in scores[1:]:
        m = jnp.maximum(m, jnp.max(s, axis=-1, keepdims=True))
    if sink is not None:
        m = jnp.maximum(m, sink)
    den = jnp.exp(sink - m) if sink is not None else 0.0
    out = None
    for s, v in zip(scores, values):
        p = jnp.exp(s - m)
        den = den + jnp.sum(p, axis=-1, keepdims=True)
        pv = _dot(p.astype(BF16), v)
        out = pv if out is None else out + pv
    return out / den


def _attn_ctx_kernel(q_ref, k_ref, v_ref, sink_ref, o_ref, *, use_sink):
    rows = q_ref.shape[0]
    qq = _stack_heads(q_ref[...].astype(BF16))
    s = _dot_nt(qq, k_ref[...].astype(BF16))
    sink = _sink_column(sink_ref, rows) if use_sink else None
    o = _softmax_pv([s], [v_ref[...].astype(BF16)], sink)
    o_ref[...] = _unstack_heads(o, rows).astype(o_ref.dtype)


def _attn_context(z, nb, seq, qblk, kblk, kstep, vblk, sink_pairs, use_sink):
    return pl.pallas_call(
        functools.partial(_attn_ctx_kernel, use_sink=use_sink),
        grid=(nb, N_HEADS // 2),
        in_specs=[pl.BlockSpec((seq, LANES), lambda b, p: (b, qblk + p)),
                  pl.BlockSpec((seq, LANES), lambda b, p: (b, kblk + kstep * p)),
                  pl.BlockSpec((seq, LANES), lambda b, p: (b, vblk + kstep * p)),
                  pl.BlockSpec((None, 2, LANES), lambda b, p: (p, 0, 0))],
        out_specs=pl.BlockSpec((seq, LANES), lambda b, p: (b, p)),
        out_shape=jax.ShapeDtypeStruct((nb * seq, BRANCH_W), BF16),
        compiler_params=_params(("arbitrary", "arbitrary")),
        name="attn_context",
    )(z, z, z, sink_pairs)


def _gqa_kernel(q_ref, k_ref, v_ref, kc_ref, vc_ref, o_ref):
    rows = q_ref.shape[0]
    qq = _stack_heads(q_ref[...])
    s_ctx = _dot_nt(qq, kc_ref[...].astype(BF16))
    s_lat = _dot_nt(qq, k_ref[...])
    o = _softmax_pv([s_ctx, s_lat], [vc_ref[...].astype(BF16), v_ref[...]], None)
    o_ref[...] = _unstack_heads(o, rows).astype(o_ref.dtype)


def _gqa_latent(z, cache, layer, nb, seq):
    tq = min(256, seq)
    past = cache.shape[3]
    nt = seq // tq
    kv = lambda which: pl.BlockSpec((None, None, None, past, LANES), lambda b, p, t: (b, layer, which, 0, 0))
    return pl.pallas_call(
        _gqa_kernel,
        grid=(nb, N_HEADS // 2, nt),
        in_specs=[pl.BlockSpec((tq, LANES), lambda b, p, t: (b * nt + t, _TILE_GQ * 4 + p)),
                  pl.BlockSpec((seq, LANES), lambda b, p, t: (b, _TILE_KV * 4)),
                  pl.BlockSpec((seq, LANES), lambda b, p, t: (b, _TILE_KV * 4 + 1)),
                  kv(0), kv(1)],
        out_specs=pl.BlockSpec((tq, LANES), lambda b, p, t: (b * nt + t, p)),
        out_shape=jax.ShapeDtypeStruct((nb * seq, BRANCH_W), BF16),
        compiler_params=_params(("arbitrary", "arbitrary", "arbitrary")),
        name="gqa_latent",
    )(z, z, z, cache, cache)


def _swa_kernel(q_ref, k_ref, v_ref, kc_ref, vc_ref, sink_ref, o_ref, *, seq, band):
    rows = q_ref.shape[0]
    q0 = pl.program_id(2) * rows
    start = pl.multiple_of(jnp.clip(q0 - SWA_WINDOW, 0, seq - band), SWA_WINDOW)
    qq = _stack_heads(q_ref[...])
    s_band = _dot_nt(qq, k_ref[pl.ds(start, band), :])
    qpos = q0 + lax.broadcasted_iota(jnp.int32, s_band.shape, 0) % rows
    kpos = start + lax.broadcasted_iota(jnp.int32, s_band.shape, 1)
    s_band = jnp.where(jnp.abs(qpos - kpos) <= SWA_WINDOW, s_band, NEG_BIG)
    s_ctx = _dot_nt(qq, kc_ref[...].astype(BF16))
    o = _softmax_pv([s_band, s_ctx], [v_ref[pl.ds(start, band), :], vc_ref[...].astype(BF16)],
                    _sink_column(sink_ref, rows))
    o_ref[...] = _unstack_heads(o, rows).astype(o_ref.dtype)


def _swa_latent(z, cache, layer, nb, seq, sink_pairs):
    tq = min(256, seq)
    band = min(tq + 2 * SWA_WINDOW, seq)
    past = cache.shape[3]
    nt = seq // tq
    kv = lambda which: pl.BlockSpec((None, None, None, past, LANES), lambda b, p, t: (b, layer, which, 0, 0))
    return pl.pallas_call(
        functools.partial(_swa_kernel, seq=seq, band=band),
        grid=(nb, N_HEADS // 2, nt),
        in_specs=[pl.BlockSpec((tq, LANES), lambda b, p, t: (b * nt + t, _TILE_SQ * 4 + p)),
                  pl.BlockSpec((seq, LANES), lambda b, p, t: (b, _TILE_KV * 4 + 2)),
                  pl.BlockSpec((seq, LANES), lambda b, p, t: (b, _TILE_KV * 4 + 3)),
                  kv(0), kv(1),
                  pl.BlockSpec((None, 2, LANES), lambda b, p, t: (p, 0, 0))],
        out_specs=pl.BlockSpec((tq, LANES), lambda b, p, t: (b * nt + t, p)),
        out_shape=jax.ShapeDtypeStruct((nb * seq, BRANCH_W), BF16),
        compiler_params=_params(("arbitrary", "arbitrary", "arbitrary")),
        name="swa_latent",
    )(z, z, z, cache, cache, sink_pairs)


def _na_kernel(q_ref, k_ref, v_ref, kc_ref, vc_ref, tb_ref, o_ref, *, grid_rows, rows_per_step):
    base = pl.program_id(2) * rows_per_step
    kc = kc_ref[...].astype(BF16)
    vc = vc_ref[...].astype(BF16)
    nwin = NA_WIN_ROWS * GRID_W

    def body(i, carry):
        r = base + i
        rs = jnp.clip(r - NA_WIN_ROWS // 2, 0, grid_rows - NA_WIN_ROWS)
        d0 = rs - r + (NA_WIN_ROWS - 1)
        qoff = pl.multiple_of(i * GRID_W, GRID_W)
        koff = pl.multiple_of(rs * GRID_W, GRID_W)
        qq = _stack_heads(q_ref[pl.ds(qoff, GRID_W), :])
        bias = jnp.concatenate([tb_ref[0, d0], tb_ref[1, d0]], axis=0)
        s_nb = _dot_nt(qq, k_ref[pl.ds(koff, nwin), :]) + bias
        s_ctx = _dot_nt(qq, kc)
        o = _softmax_pv([s_nb, s_ctx], [v_ref[pl.ds(koff, nwin), :], vc], None)
        o_ref[pl.ds(qoff, GRID_W), :] = _unstack_heads(o, GRID_W).astype(o_ref.dtype)
        return carry

    lax.fori_loop(0, rows_per_step, body, 0)


def _na_bias_table(rpb):
    qc = np.arange(GRID_W)[:, None]
    kc = np.arange(GRID_W)[None, :]
    cs = np.clip(qc - NA_WIN_COLS // 2, 0, GRID_W - NA_WIN_COLS)
    valid = (kc >= cs) & (kc < cs + NA_WIN_COLS)
    cidx = np.clip(kc - qc + NA_WIN_COLS - 1, 0, 2 * NA_WIN_COLS - 2)
    ridx = np.arange(NA_WIN_ROWS)[:, None] + np.arange(NA_WIN_ROWS)[None, :]
    t = rpb.astype(F32)[:, ridx][:, :, :, cidx]
    t = jnp.where(valid[None, None, None], t, NEG_BIG)
    return jnp.transpose(t, (0, 1, 3, 2, 4)).reshape(rpb.shape[0], NA_WIN_ROWS, GRID_W, NA_WIN_ROWS * GRID_W)


def _na_latent(z, cache, layer, nb, seq, table):
    grid_rows = seq // GRID_W
    rps = 8
    tq = rps * GRID_W
    nt = seq // tq
    past = cache.shape[3]
    kv = lambda which: pl.BlockSpec((None, None, None, past, LANES), lambda b, p, t: (b, layer, which, 0, p))
    return pl.pallas_call(
        functools.partial(_na_kernel, grid_rows=grid_rows, rows_per_step=rps),
        grid=(nb, N_HEADS // 2, nt),
        in_specs=[pl.BlockSpec((tq, LANES), lambda b, p, t: (b * nt + t, p)),
                  pl.BlockSpec((seq, LANES), lambda b, p, t: (b, 4 + p)),
                  pl.BlockSpec((seq, LANES), lambda b, p, t: (b, 8 + p)),
                  kv(0), kv(1),
                  pl.BlockSpec((2, NA_WIN_ROWS, GRID_W, NA_WIN_ROWS * GRID_W), lambda b, p, t: (p, 0, 0, 0))],
        out_specs=pl.BlockSpec((tq, LANES), lambda b, p, t: (b * nt + t, p)),
        out_shape=jax.ShapeDtypeStruct((nb * seq, BRANCH_W), BF16),
        compiler_params=_params(("arbitrary", "arbitrary", "arbitrary")),
        name="na_latent",
    )(z, z, z, cache, cache, table)


def _mlstm_chunk(d, ch, carry, q_ref, k_ref, v_ref, gs_ref, gt_ref):
    cmat, nvec, m = carry
    off = pl.multiple_of(ch * ML_CHUNK, ML_CHUNK)
    q = q_ref[pl.ds(off, ML_CHUNK), :].astype(BF16)
    k = k_ref[pl.ds(off, ML_CHUNK), :].astype(BF16)
    v = v_ref[pl.ds(off, ML_CHUNK), :].astype(BF16)
    g = gs_ref[pl.ds(off, ML_CHUNK), :]
    i_col, b_col = g[:, d:d + 1], g[:, 2 + d:3 + d]
    i_row = gt_ref[d, pl.ds(ch, 1), :]
    b_row = gt_ref[2 + d, pl.ds(ch, 1), :]
    total = b_row[:, ML_CHUNK - 1:ML_CHUNK] if d == 0 else b_row[:, 0:1]
    t_idx = lax.broadcasted_iota(jnp.int32, (ML_CHUNK, ML_CHUNK), 0)
    s_idx = lax.broadcasted_iota(jnp.int32, (ML_CHUNK, ML_CHUNK), 1)
    causal = (s_idx <= t_idx) if d == 0 else (s_idx >= t_idx)
    dm = jnp.where(causal, b_col - b_row + i_row, -jnp.inf)
    inter = b_col + m
    m_t = jnp.maximum(inter, jnp.max(dm, axis=-1, keepdims=True))
    w_intra = jnp.exp(dm - m_t)
    w_inter = jnp.exp(inter - m_t)
    s = _dot_nt(q, k) * w_intra
    num = w_inter * _dot(q, cmat.astype(BF16)) + _dot(s.astype(BF16), v)
    den = w_inter * jnp.sum(q.astype(F32) * nvec, axis=-1, keepdims=True) + jnp.sum(s, axis=-1, keepdims=True)
    h = num / jnp.maximum(jnp.abs(den), jnp.exp(-m_t))
    gcol = total - b_col + i_col
    m_new = jnp.maximum(total + m, jnp.max(gcol, axis=0, keepdims=True))
    kw = k.astype(F32) * jnp.exp(gcol - m_new)
    decay = jnp.exp(total + m - m_new)
    c_new = decay * cmat + _dot(kw.T.astype(BF16), v)
    n_new = decay * nvec + jnp.sum(kw, axis=0, keepdims=True)
    return (c_new, n_new, m_new), h


def _mlstm_kernel(q_ref, k_ref, v_ref, g_ref, gt_ref, c0_ref, n0_ref, m0_ref, h_ref, c_out, n_out, m_out, gs_scr, *, nc):
    head = pl.program_id(1)
    gs_scr[...] = pltpu.roll(g_ref[...], ((ML_HEADS - head) % ML_HEADS) * 32, 1)
    half = nc // 2

    def init(d):
        return (c0_ref[d], n0_ref[d:d + 1, :], m0_ref[d:d + 1, 0:1])

    def step(c, carry, first):
        cf, cb = carry
        chunks = (c, nc - 1 - c)
        new = []
        for d, cr in enumerate((cf, cb)):
            nxt, h = _mlstm_chunk(d, chunks[d], cr, q_ref, k_ref, v_ref, gs_scr, gt_ref)
            rows = pl.ds(pl.multiple_of(chunks[d] * ML_CHUNK, ML_CHUNK), ML_CHUNK)
            if first:
                h_ref[rows, :] = h
            else:
                h_ref[rows, :] = h_ref[rows, :] + h
            new.append(nxt)
        return tuple(new)

    carry = (init(0), init(1))
    carry = lax.fori_loop(0, half, lambda c, cr: step(c, cr, True), carry)
    carry = lax.fori_loop(half, nc, lambda c, cr: step(c, cr, False), carry)
    for d in range(2):
        c_out[d] = carry[d][0]
        n_out[d:d + 1, :] = carry[d][1]
        m_out[d:d + 1, :] = jnp.broadcast_to(carry[d][2], (1, LANES))


def _mlstm(z, gates, nb, seq, c0, n0, m0):
    nc = seq // ML_CHUNK
    gt = gates.reshape(nb, seq, ML_HEADS, 32)[..., :4]
    gt = jnp.transpose(gt, (0, 2, 3, 1)).reshape(nb, ML_HEADS, 4, nc, ML_CHUNK)
    qkv = lambda blk: pl.BlockSpec((seq, LANES), lambda b, h: (b, blk * 4 + h))
    st = pl.BlockSpec((None, None, 2, ML_DH), lambda b, h: (b, h, 0, 0))
    cst = pl.BlockSpec((None, None, 2, ML_DH, ML_DH), lambda b, h: (b, h, 0, 0, 0))
    return pl.pallas_call(
        functools.partial(_mlstm_kernel, nc=nc),
        grid=(nb, ML_HEADS),
        in_specs=[qkv(6), qkv(7), qkv(8),
                  pl.BlockSpec((seq, LANES), lambda b, h: (b, 0)),
                  pl.BlockSpec((None, None, 4, nc, ML_CHUNK), lambda b, h: (b, h, 0, 0, 0)),
                  cst, st, st],
        out_specs=[pl.BlockSpec((seq, LANES), lambda b, h: (b, h)), cst, st, st],
        out_shape=[jax.ShapeDtypeStruct((nb * seq, ML_HEADS * ML_DH), F32),
                   jax.ShapeDtypeStruct((nb, ML_HEADS, 2, ML_DH, ML_DH), F32),
                   jax.ShapeDtypeStruct((nb, ML_HEADS, 2, ML_DH), F32),
                   jax.ShapeDtypeStruct((nb, ML_HEADS, 2, ML_DH), F32)],
        scratch_shapes=[pltpu.VMEM((seq, LANES), F32)],
        compiler_params=_params(("arbitrary", "arbitrary")),
        name="mlstm",
    )(z, z, z, gates, gt, c0, n0, m0)


def _merge_kernel(ona_ref, ogqa_ref, oswa_ref, hml_ref, mo_ref, g0_ref, g1_ref, g2_ref, g3_ref, x_ref, gt_ref,
                  ng_ref, wb_ref, wo_ref, o_ref):
    y = hml_ref[...] * jax.nn.sigmoid(mo_ref[...].astype(F32))
    parts = []
    for c in range(ML_HEADS):
        yc = y[:, c * ML_DH:(c + 1) * ML_DH]
        parts.append(yc * lax.rsqrt(jnp.mean(yc * yc, axis=-1, keepdims=True) + NORM_EPS))
    yml = (jnp.concatenate(parts, axis=1) * ng_ref[...]).astype(BF16)
    merged = None
    for i, (o, g) in enumerate(zip((ona_ref[...], ogqa_ref[...], oswa_ref[...], yml),
                                   (g0_ref, g1_ref, g2_ref, g3_ref))):
        term = g[...].astype(F32) * _dot(o, wb_ref[i])
        merged = term if merged is None else merged + term
    out = _dot(merged.astype(BF16), wo_ref[...])
    o_ref[...] = x_ref[...] + gt_ref[...] * out


def _merge(o_na, o_gqa, o_swa, h_ml, z, x, gate, lw, seq_per_mod):
    t = x.shape[0]
    tm = min(256, t)
    per_mod = seq_per_mod // tm
    row = lambda w: pl.BlockSpec((tm, w), lambda i: (i, 0))
    gcol = lambda blk: pl.BlockSpec((tm, D_MODEL), lambda i: (i, blk))
    return pl.pallas_call(
        _merge_kernel,
        grid=(t // tm,),
        in_specs=[row(BRANCH_W), row(BRANCH_W), row(BRANCH_W), row(BRANCH_W),
                  pl.BlockSpec((tm, TN), lambda i: (i, 9)),
                  gcol(5), gcol(6), gcol(7), gcol(8),
                  row(D_MODEL),
                  pl.BlockSpec((None, 1, D_MODEL), lambda i: (i // per_mod, 0, 0)),
                  pl.BlockSpec((1, BRANCH_W), lambda i: (0, 0)),
                  pl.BlockSpec((N_BRANCH, BRANCH_W, D_MODEL), lambda i: (0, 0, 0)),
                  pl.BlockSpec((D_MODEL, D_MODEL), lambda i: (0, 0))],
        out_specs=row(D_MODEL),
        out_shape=jax.ShapeDtypeStruct((t, D_MODEL), F32),
        compiler_params=_params(("arbitrary",)),
        name="merge",
    )(o_na, o_gqa, o_swa, h_ml, z, z, z, z, z, x, gate, lw["ml_gain"], lw["w_branch"], lw["w_out"])


def _ffn_kernel(x_ref, g_ref, sh_ref, sc_ref, gt_ref, wg_ref, wu_ref, w2_ref, fg_ref, o_ref, h_scr, acc_scr, *, final):
    k = pl.program_id(1)

    @pl.when(k == 0)
    def _():
        x = x_ref[...]
        y = x * lax.rsqrt(jnp.mean(x * x, axis=-1, keepdims=True) + NORM_EPS) * g_ref[...]
        h_scr[...] = (y * (1.0 + sc_ref[...]) + sh_ref[...]).astype(BF16)
        acc_scr[...] = jnp.zeros_like(acc_scr)

    h = h_scr[...]
    a = _dot(h, wg_ref[...])
    u = _dot(h, wu_ref[...])
    act = (a * jax.nn.sigmoid(a) * u).astype(BF16)
    acc_scr[...] += _dot(act, w2_ref[...])

    @pl.when(k == pl.num_programs(1) - 1)
    def _():
        xn = x_ref[...] + gt_ref[...] * acc_scr[...]
        if final:
            xn = xn * lax.rsqrt(jnp.mean(xn * xn, axis=-1, keepdims=True) + NORM_EPS) * fg_ref[...]
        o_ref[...] = xn


def _ffn(x, gain, shift, scale, gate, lw, final_gain, seq_per_mod, final):
    t = x.shape[0]
    tm = min(512, t)
    tf = D_FF // 2
    nk = D_FF // tf
    per_mod = seq_per_mod // tm
    mod = pl.BlockSpec((None, 1, D_MODEL), lambda i, k: (i // per_mod, 0, 0))
    vec = pl.BlockSpec((1, D_MODEL), lambda i, k: (0, 0))
    return pl.pallas_call(
        functools.partial(_ffn_kernel, final=final),
        grid=(t // tm, nk),
        in_specs=[pl.BlockSpec((tm, D_MODEL), lambda i, k: (i, 0)), vec, mod, mod, mod,
                  pl.BlockSpec((D_MODEL, tf), lambda i, k: (0, k)),
                  pl.BlockSpec((D_MODEL, tf), lambda i, k: (0, nk + k)),
                  pl.BlockSpec((tf, D_MODEL), lambda i, k: (k, 0)),
                  vec],
        out_specs=pl.BlockSpec((tm, D_MODEL), lambda i, k: (i, 0)),
        out_shape=jax.ShapeDtypeStruct((t, D_MODEL), F32),
        scratch_shapes=[pltpu.VMEM((tm, D_MODEL), BF16), pltpu.VMEM((tm, D_MODEL), F32)],
        compiler_params=_params(("arbitrary", "arbitrary")),
        name="ffn",
    )(x, gain, shift, scale, gate, lw["w_ffn_in"], lw["w_ffn_in"], lw["w_ffn_out"], final_gain)


def _rope_tables(seq):
    half = DH // 2
    freqs = ROPE_BASE ** (-jnp.arange(0, half, 2, dtype=F32) / half)
    t = jnp.arange(seq)
    lane = np.arange(DH)
    pos = jnp.where(jnp.asarray(lane < half)[None, :], (t // GRID_W)[:, None], (t % GRID_W)[:, None]).astype(F32)
    ang = pos * freqs[lane % (half // 2)][None, :]
    cos, sin = jnp.cos(ang), jnp.sin(ang)
    first = jnp.asarray((lane % half) < half // 2)[None, :]
    sa = jnp.where(first, -sin, 0.0)
    sb = jnp.where(first, 0.0, sin)
    two = lambda a: jnp.concatenate([a, a], axis=1)
    return two(cos), two(sa), two(sb)


def _layer_weights(l, w_in, b_in, gqa_q_g, gqa_k_g, mlstm_norm_g, w_branch, w_out, w_ffn_in, w_ffn_out):
    pm = np.kron(np.eye(TN // DH, dtype=np.float32), np.full((DH, DH), 1.0 / DH, np.float32))
    hperm = np.concatenate([h * DH + np.arange(DH) for h in _QPERM])
    wb = w_branch[l]
    wb = jnp.stack([wb[0], wb[1][hperm], wb[2][hperm], wb[3]])
    return {
        "w_main": w_in[l][:, _COLS].astype(BF16),
        "b_main": b_in[l][_COLS].reshape(1, N_MAIN),
        "colscale": jnp.asarray(_COLSCALE).reshape(1, N_MAIN),
        "w_if": (w_in[l][:, _GCOLS] * _GMASK[None, :]).astype(BF16),
        "b_if": (b_in[l][_GCOLS] * _GMASK).reshape(1, LANES),
        "pmat": jnp.asarray(pm, BF16),
        "q_gain": (jnp.tile(gqa_q_g[l], N_HEADS) * DH ** -0.5).reshape(1, TN),
        "k_gain": jnp.tile(gqa_k_g[l], 2).reshape(1, LANES),
        "ml_gain": mlstm_norm_g[l].reshape(1, BRANCH_W),
        "w_branch": wb.astype(BF16),
        "w_out": w_out[l].astype(BF16),
        "w_ffn_in": w_ffn_in[l].astype(BF16),
        "w_ffn_out": w_ffn_out[l].astype(BF16),
    }


def _sink_pairs(sink):
    s = sink.astype(F32)[np.asarray(_QPERM)].reshape(N_HEADS // 2, 2, 1)
    return jnp.broadcast_to(s, (N_HEADS // 2, 2, LANES))


def _state_layout(c, n, m):
    return (jnp.transpose(c, (0, 2, 1, 3, 4)), jnp.transpose(n, (0, 2, 1, 3)),
            jnp.broadcast_to(jnp.transpose(m, (0, 2, 1))[..., None], n.shape[:1] + (ML_HEADS, 2, ML_DH)))


def kernel(x_prompt, x_sample, cache_na_kv, cache_gqa_kv, cache_swa_kv, state_mlstm_C, state_mlstm_n, state_mlstm_m, c, c_ctx, w_mod, b_mod, norm1_g, norm2_g, w_in, b_in, na_rpb, gqa_q_g, gqa_k_g, swa_sink, mlstm_norm_g, w_branch, w_out, w_ffn_in, w_ffn_out, final_norm_g):
    bp, sp, _ = x_prompt.shape
    bs, ss, _ = x_sample.shape
    depth = w_mod.shape[0]
    past = cache_na_kv.shape[3]

    cvecs = jnp.concatenate([c, c_ctx[None, :], jnp.zeros((16 - bs - 1, D_MODEL), F32)], axis=0)
    mods = _modulation(cvecs, w_mod, b_mod).reshape(depth, 16, 6, D_MODEL)

    rope = _rope_tables(ss)
    rope_ctx = tuple(a[:min(1024, bp * sp)] for a in rope)
    cache_na = cache_na_kv.reshape(bs, depth, 2, past, N_HEADS * DH)
    cache_gqa = cache_gqa_kv.reshape(bs, depth, 2, past, LANES)
    cache_swa = cache_swa_kv.reshape(bs, depth, 2, past, LANES)
    zero_state = (jnp.zeros((bp, ML_HEADS, 2, ML_DH, ML_DH), F32), jnp.zeros((bp, ML_HEADS, 2, ML_DH), F32),
                  jnp.zeros((bp, ML_HEADS, 2, ML_DH), F32))
    final_gain = final_norm_g.reshape(1, D_MODEL)

    xp = x_prompt.reshape(bp * sp, D_MODEL)
    xs = x_sample.reshape(bs * ss, D_MODEL)
    na_list, gqa_list, swa_list, c_list, n_list, m_list = [], [], [], [], [], []
    for l in range(depth):
        lw = _layer_weights(l, w_in, b_in, gqa_q_g, gqa_k_g, mlstm_norm_g, w_branch, w_out, w_ffn_in, w_ffn_out)
        g1 = norm1_g[l].reshape(1, D_MODEL)
        g2 = norm2_g[l].reshape(1, D_MODEL)
        sinks = _sink_pairs(swa_sink[l])
        last = l == depth - 1

        m = [mods[l, bs:bs + 1, i].reshape(1, 1, D_MODEL) for i in range(6)]
        z, gates = _in_projection(xp, g1, m[0], m[1], lw, rope_ctx, sp, False, F32)
        o_na = _attn_context(z, bp, sp, 0, 4, 1, 8, sinks, False)
        o_gqa = _attn_context(z, bp, sp, _TILE_GQ * 4, _TILE_KV * 4, 0, _TILE_KV * 4 + 1, sinks, False)
        o_swa = _attn_context(z, bp, sp, _TILE_SQ * 4, _TILE_KV * 4 + 2, 0, _TILE_KV * 4 + 3, sinks, True)
        h_ml, c_fin, n_fin, m_fin = _mlstm(z, gates, bp, sp, *zero_state)
        xp = _merge(o_na, o_gqa, o_swa, h_ml, z, xp, m[2], lw, bp * sp)
        xp = _ffn(xp, g2, m[3], m[4], m[5], lw, final_gain, bp * sp, last)
        heads = lambda off, n: z[:, off:off + n * DH].reshape(bp, sp, n, DH)
        na_list.append(jnp.stack([heads(512, N_HEADS), heads(1024, N_HEADS)], axis=1))
        kvoff = _TILE_KV * TN
        gqa_list.append(jnp.stack([heads(kvoff, 2), heads(kvoff + 128, 2)], axis=1))
        swa_list.append(jnp.stack([heads(kvoff + 256, 2), heads(kvoff + 384, 2)], axis=1))
        c_list.append(jnp.transpose(c_fin, (0, 2, 1, 3, 4)))
        n_list.append(jnp.transpose(n_fin, (0, 2, 1, 3)))
        m_list.append(jnp.transpose(m_fin[..., 0], (0, 2, 1)))

        m = [mods[l, :bs, i].reshape(bs, 1, D_MODEL) for i in range(6)]
        z, gates = _in_projection(xs, g1, m[0], m[1], lw, rope, ss, True, BF16)
        o_na = _na_latent(z, cache_na, l, bs, ss, _na_bias_table(na_rpb[l]))
        o_gqa = _gqa_latent(z, cache_gqa, l, bs, ss)
        o_swa = _swa_latent(z, cache_swa, l, bs, ss, sinks)
        st = _state_layout(state_mlstm_C[:, l].astype(F32), state_mlstm_n[:, l].astype(F32),
                           state_mlstm_m[:, l].astype(F32))
        h_ml = _mlstm(z, gates, bs, ss, *st)[0]
        xs = _merge(o_na, o_gqa, o_swa, h_ml, z, xs, m[2], lw, ss)
        xs = _ffn(xs, g2, m[3], m[4], m[5], lw, final_gain, ss, last)

    return (xp.reshape(bp, sp, D_MODEL), xs.reshape(bs, ss, D_MODEL),
            jnp.stack(na_list, axis=1), jnp.stack(gqa_list, axis=1), jnp.stack(swa_list, axis=1),
            jnp.stack(c_list, axis=1), jnp.stack(n_list, axis=1), jnp.stack(m_list, axis=1))
```
